```python
import jax, jax.numpy as jnp
from jax import lax
import numpy as np

D_MODEL = 1024
BATCH = 2
SEQ = 8192
DEPTH = 2
DEC_BATCH = 32
DEC_SEQ = 16
PAST_LEN = 1024

CHUNK = 64
N_HEADS = 16
HEAD_DIM = D_MODEL // N_HEADS
N_A_LAYERS = DEPTH // 2
N_B_LAYERS = DEPTH - N_A_LAYERS
BAND_CHUNKS = 8
BAND_PAST = BAND_CHUNKS * CHUNK
REL_CLIP = 128
N_REL = 2 * REL_CLIP + 1
D_FF = ((8 * D_MODEL // 3 + 255) // 256) * 256
Q_BLOCK = 128
EPS = 1e-6
NEG_INF = -1e30

kernel_name = "fox_yoco_chunkband_stream_step"


def rms_norm(x, g):
    xf = x.astype(jnp.float32)
    y = xf * lax.rsqrt(jnp.mean(xf * xf, axis=-1, keepdims=True) + EPS)
    return (y * g.astype(jnp.float32)).astype(x.dtype)


def swiglu(x, w_gate_up, w_down):
    gate, up = jnp.split(x @ w_gate_up, 2, axis=-1)
    return (jax.nn.silu(gate) * up) @ w_down


def split_heads(t):
    return t.reshape(t.shape[0], t.shape[1], N_HEADS, HEAD_DIM)


def merge_heads(t):
    return t.reshape(t.shape[0], t.shape[1], D_MODEL)


def fox_project(h, w_in, b_f):
    proj = h @ w_in
    q, k, v, f = jnp.split(proj, [D_MODEL, 2 * D_MODEL, 3 * D_MODEL], axis=-1)
    logf = jax.nn.log_sigmoid((f + b_f).astype(jnp.float32))
    return split_heads(q), split_heads(k), split_heads(v), logf


def fox_attention_prompt(q, k, v, logf):
    bsz, s_len = q.shape[0], q.shape[1]
    scale = HEAD_DIM ** -0.5
    cum = jnp.transpose(jnp.cumsum(logf, axis=1), (0, 2, 1))
    key_pos = jnp.arange(s_len)

    def block(i):
        start = i * Q_BLOCK
        qb = lax.dynamic_slice_in_dim(q, start, Q_BLOCK, axis=1)
        cq = lax.dynamic_slice_in_dim(cum, start, Q_BLOCK, axis=2)
        s = jnp.einsum('bqhd,bkhd->bhqk', qb, k).astype(jnp.float32) * scale
        s = s + cq[..., :, None] - cum[..., None, :]
        qpos = start + jnp.arange(Q_BLOCK)
        s = jnp.where(key_pos[None, :] <= qpos[:, None], s, NEG_INF)
        p = jax.nn.softmax(s, axis=-1).astype(v.dtype)
        return jnp.einsum('bhqk,bkhd->bqhd', p, v)

    out = lax.map(block, jnp.arange(s_len // Q_BLOCK))
    return jnp.moveaxis(out, 0, 1).reshape(bsz, s_len, N_HEADS, HEAD_DIM)


def fox_attention_sample(q, k_all, v_all, logf_all, past):
    t_len = q.shape[1]
    scale = HEAD_DIM ** -0.5
    cum = jnp.transpose(jnp.cumsum(logf_all, axis=1), (0, 2, 1))
    s = jnp.einsum('bqhd,bkhd->bhqk', q, k_all).astype(jnp.float32) * scale
    s = s + cum[..., past:, None] - cum[..., None, :]
    qpos = past + jnp.arange(t_len)
    kpos = jnp.arange(k_all.shape[1])
    s = jnp.where(kpos[None, :] <= qpos[:, None], s, NEG_INF)
    p = jax.nn.softmax(s, axis=-1).astype(v_all.dtype)
    return jnp.einsum('bhqk,bkhd->bqhd', p, v_all)


def rel_index(dist):
    return jnp.clip(dist, -REL_CLIP, REL_CLIP) + REL_CLIP


def band_attention_prompt(q, k, v, rel_bias):
    bsz, s_len = q.shape[0], q.shape[1]
    scale = HEAD_DIM ** -0.5
    band = BAND_PAST + CHUNK
    pad = ((0, 0), (BAND_PAST, 0), (0, 0), (0, 0))
    kp = jnp.pad(k, pad)
    vp = jnp.pad(v, pad)
    qi = jnp.arange(CHUNK)
    kj = jnp.arange(band)
    bias = rel_bias[:, rel_index(qi[:, None] + BAND_PAST - kj[None, :])].astype(jnp.float32)

    def chunk(c):
        qs = lax.dynamic_slice_in_dim(q, c * CHUNK, CHUNK, axis=1)
        ks = lax.dynamic_slice_in_dim(kp, c * CHUNK, band, axis=1)
        vs = lax.dynamic_slice_in_dim(vp, c * CHUNK, band, axis=1)
        s = jnp.einsum('bqhd,bkhd->bhqk', qs, ks).astype(jnp.float32) * scale + bias
        key_pos = c * CHUNK - BAND_PAST + kj
        s = jnp.where((key_pos >= 0)[None, :], s, NEG_INF)
        p = jax.nn.softmax(s, axis=-1).astype(vs.dtype)
        return jnp.einsum('bhqk,bkhd->bqhd', p, vs)

    out = lax.map(chunk, jnp.arange(s_len // CHUNK))
    return jnp.moveaxis(out, 0, 1).reshape(bsz, s_len, N_HEADS, HEAD_DIM)


def band_attention_sample(q, k_all, v_all, rel_bias, past):
    t_len, l_len = q.shape[1], k_all.shape[1]
    w_len = l_len - t_len
    scale = HEAD_DIM ** -0.5
    qpos = past + jnp.arange(t_len)
    kpos = past - w_len + jnp.arange(l_len)
    qc = qpos // CHUNK
    kc = kpos // CHUNK
    mask = (kc[None, :] <= qc[:, None]) & (kc[None, :] >= qc[:, None] - BAND_CHUNKS)
    bias = rel_bias[:, rel_index(qpos[:, None] - kpos[None, :])].astype(jnp.float32)
    s = jnp.einsum('bqhd,bkhd->bhqk', q, k_all).astype(jnp.float32) * scale + bias
    s = jnp.where(mask, s, NEG_INF)
    p = jax.nn.softmax(s, axis=-1).astype(v_all.dtype)
    return jnp.einsum('bhqk,bkhd->bqhd', p, v_all)


def shared_kv(h, g_kv, w_kv):
    kb, vb = jnp.split(rms_norm(h, g_kv) @ w_kv, 2, axis=-1)
    return split_heads(kb), split_heads(vb)


def setup_inputs(seed: int = 0) -> dict:
    key = jax.random.key(seed)
    ks = jax.random.split(key, 24)
    f32 = jnp.float32
    d_in = D_MODEL ** -0.5
    b_len = min(BAND_PAST, PAST_LEN)
    nrm = lambda k, shape, s: jax.random.normal(k, shape, f32) * s
    return {
        "x_prompt": nrm(ks[0], (BATCH, SEQ, D_MODEL), 1.0),
        "x_sample": nrm(ks[1], (DEC_BATCH, DEC_SEQ, D_MODEL), 1.0),
        "cache_a_k": nrm(ks[2], (N_A_LAYERS, DEC_BATCH, PAST_LEN, N_HEADS, HEAD_DIM), 1.0),
        "cache_a_v": nrm(ks[3], (N_A_LAYERS, DEC_BATCH, PAST_LEN, N_HEADS, HEAD_DIM), 1.0),
        "cache_a_logf": jax.nn.log_sigmoid(3.0 + nrm(ks[4], (N_A_LAYERS, DEC_BATCH, PAST_LEN, N_HEADS), 1.0)),
        "cache_b_k": nrm(ks[5], (DEC_BATCH, b_len, N_HEADS, HEAD_DIM), 1.0),
        "cache_b_v": nrm(ks[6], (DEC_BATCH, b_len, N_HEADS, HEAD_DIM), 1.0),
        "g_attn": 1.0 + nrm(ks[7], (DEPTH, D_MODEL), 0.02),
        "w_a_in": nrm(ks[8], (N_A_LAYERS, D_MODEL, 3 * D_MODEL + N_HEADS), d_in),
        "b_a_f": 3.0 + nrm(ks[9], (N_A_LAYERS, N_HEADS), 0.5),
        "w_a_out": nrm(ks[10], (N_A_LAYERS, D_MODEL, D_MODEL), d_in),
        "w_b_q": nrm(ks[11], (N_B_LAYERS, D_MODEL, D_MODEL), d_in),
        "rel_bias": nrm(ks[12], (N_B_LAYERS, N_HEADS, N_REL), 0.5),
        "w_b_out": nrm(ks[13], (N_B_LAYERS, D_MODEL, D_MODEL), d_in),
        "g_kv": 1.0 + nrm(ks[14], (D_MODEL,), 0.02),
        "w_kv": nrm(ks[15], (D_MODEL, 2 * D_MODEL), d_in),
        "g_ffn": 1.0 + nrm(ks[16], (DEPTH, D_MODEL), 0.02),
        "w_gate_up": nrm(ks[17], (DEPTH, D_MODEL, 2 * D_FF), d_in),
        "w_down": nrm(ks[18], (DEPTH, D_FF, D_MODEL), D_FF ** -0.5),
        "g_final": 1.0 + nrm(ks[19], (D_MODEL,), 0.02),
    }


def reference(x_prompt, x_sample, cache_a_k, cache_a_v, cache_a_logf, cache_b_k, cache_b_v,
              g_attn, w_a_in, b_a_f, w_a_out, w_b_q, rel_bias, w_b_out, g_kv, w_kv,
              g_ffn, w_gate_up, w_down, g_final):
    past = cache_a_k.shape[2]

    h = x_prompt
    ak_p, av_p, alf_p = [], [], []
    kb_p = vb_p = None
    for l in range(DEPTH):
        if l < N_A_LAYERS:
            q, k, v, lf = fox_project(rms_norm(h, g_attn[l]), w_a_in[l], b_a_f[l])
            h = h + merge_heads(fox_attention_prompt(q, k, v, lf)) @ w_a_out[l]
            ak_p.append(k); av_p.append(v); alf_p.append(lf)
        else:
            j = l - N_A_LAYERS
            if j == 0:
                kb_p, vb_p = shared_kv(h, g_kv, w_kv)
            q = split_heads(rms_norm(h, g_attn[l]) @ w_b_q[j])
            h = h + merge_heads(band_attention_prompt(q, kb_p, vb_p, rel_bias[j])) @ w_b_out[j]
        h = h + swiglu(rms_norm(h, g_ffn[l]), w_gate_up[l], w_down[l])
    y_prompt = rms_norm(h, g_final)
    keep_p = min(BAND_PAST, kb_p.shape[1])
    new_b_k_p = kb_p[:, kb_p.shape[1] - keep_p:]
    new_b_v_p = vb_p[:, vb_p.shape[1] - keep_p:]

    h = x_sample
    ak_s, av_s, alf_s = [], [], []
    kb_all = vb_all = None
    for l in range(DEPTH):
        if l < N_A_LAYERS:
            q, k, v, lf = fox_project(rms_norm(h, g_attn[l]), w_a_in[l], b_a_f[l])
            k_all = jnp.concatenate([cache_a_k[l].astype(k.dtype), k], axis=1)
            v_all = jnp.concatenate([cache_a_v[l].astype(v.dtype), v], axis=1)
            lf_all = jnp.concatenate([cache_a_logf[l].astype(jnp.float32), lf], axis=1)
            h = h + merge_heads(fox_attention_sample(q, k_all, v_all, lf_all, past)) @ w_a_out[l]
            ak_s.append(k); av_s.append(v); alf_s.append(lf)
        else:
            j = l - N_A_LAYERS
            if j == 0:
                kb_new, vb_new = shared_kv(h, g_kv, w_kv)
                kb_all = jnp.concatenate([cache_b_k.astype(kb_new.dtype), kb_new], axis=1)
                vb_all = jnp.concatenate([cache_b_v.astype(vb_new.dtype), vb_new], axis=1)
            q = split_heads(rms_norm(h, g_attn[l]) @ w_b_q[j])
            h = h + merge_heads(band_attention_sample(q, kb_all, vb_all, rel_bias[j], past)) @ w_b_out[j]
        h = h + swiglu(rms_norm(h, g_ffn[l]), w_gate_up[l], w_down[l])
    y_sample = rms_norm(h, g_final)
    keep_s = cache_b_k.shape[1]
    new_b_k_s = kb_all[:, kb_all.shape[1] - keep_s:]
    new_b_v_s = vb_all[:, vb_all.shape[1] - keep_s:]

    return (y_prompt, y_sample,
            jnp.stack(ak_p, 0), jnp.stack(av_p, 0), jnp.stack(alf_p, 0), new_b_k_p, new_b_v_p,
            jnp.stack(ak_s, 0), jnp.stack(av_s, 0), jnp.stack(alf_s, 0), new_b_k_s, new_b_v_s)
```

```python
import functools

import numpy as np
import jax
import jax.numpy as jnp
from jax import lax
from jax.experimental import pallas as pl
from jax.experimental.pallas import tpu as pltpu

F32 = jnp.float32
BF16 = jnp.bfloat16

LANES = 128
CHUNK = 64
BAND_CHUNKS = 8
BAND_PAST = BAND_CHUNKS * CHUNK
REL_CLIP = 128
N_REL = 2 * REL_CLIP + 1
EPS = 1e-6
NEG_INF = -1e30
VMEM_LIMIT_BYTES = 56 * 1024 * 1024

BIAS_LANES_PER_HEAD = 8
ONE_LANE = 48


def _split3(x):
    hi = x.astype(BF16)
    r1 = x - hi.astype(F32)
    mid = r1.astype(BF16)
    lo = (r1 - mid.astype(F32)).astype(BF16)
    return hi, mid, lo


def _dot(a, b):
    return jnp.dot(a, b, preferred_element_type=F32)


def _dot_nt(a, b):
    return lax.dot_general(a, b, (((1,), (1,)), ((), ())), preferred_element_type=F32)


def _dot3(a_bf16, x_f32):
    hi, mid, lo = _split3(x_f32)
    return _dot(a_bf16, hi) + _dot(a_bf16, mid) + _dot(a_bf16, lo)


def _rms_scale(x):
    return x * lax.rsqrt(jnp.mean(x * x, axis=-1, keepdims=True) + EPS)


def _log_sigmoid(x):
    return -(jnp.maximum(-x, 0.0) + jnp.log1p(jnp.exp(-jnp.abs(x))))


def _div_pow2(x, n):
    assert n & (n - 1) == 0
    return x >> (n.bit_length() - 1)


def _mod_pow2(x, n):
    assert n & (n - 1) == 0
    return x & (n - 1)


def _lower_tri(n):
    r = lax.broadcasted_iota(jnp.int32, (n, n), 0)
    c = lax.broadcasted_iota(jnp.int32, (n, n), 1)
    return jnp.where(r >= c, 1.0, 0.0).astype(BF16)


def _pack3(c):
    hi, mid, lo = _split3(c)
    lane = lax.broadcasted_iota(jnp.int32, c.shape, 1)
    one = jnp.where(lane == ONE_LANE, 1.0, 0.0)
    packed = (hi.astype(F32) + pltpu.roll(mid.astype(F32), 16, 1)
              + pltpu.roll(lo.astype(F32), 32, 1) + one)
    return packed.astype(BF16)


def _bias_select_matrices(n_heads):
    pq = np.zeros((LANES, LANES), np.float32)
    pk = np.zeros((LANES, LANES), np.float32)
    for h in range(n_heads):
        base = h * BIAS_LANES_PER_HEAD
        for part in range(3):
            pq[part * 16 + h, base + part] = 1.0
            pq[ONE_LANE, base + 3 + part] = 1.0
            pk[ONE_LANE, base + part] = 1.0
            pk[part * 16 + h, base + 3 + part] = -1.0
    return jnp.asarray(pq, BF16), jnp.asarray(pk, BF16)


def _cumsum_rows(lf, carry, tri, blk):
    outs = []
    for c in range(lf.shape[0] // blk):
        cs = _dot3(tri, lf[c * blk:(c + 1) * blk]) + carry
        carry = cs[blk - 1:blk, :]
        outs.append(cs)
    return (outs[0] if len(outs) == 1 else jnp.concatenate(outs, axis=0)), carry


def _const_spec(shape):
    zeros = (0,) * len(shape)
    return pl.BlockSpec(shape, lambda *_: zeros, pipeline_mode=pl.Buffered(1))


def _params(n_axes):
    return pltpu.CompilerParams(dimension_semantics=("arbitrary",) * n_axes,
                                vmem_limit_bytes=VMEM_LIMIT_BYTES)


CUMSUM_BLOCK = 256


def _proj_a_kernel(*refs, d_model, n_heads, with_bias):
    if with_bias:
        (x_ref, g_ref, w_ref, wf_ref, bf_ref, pq_ref, pk_ref,
         q_ref, k_ref, v_ref, lf_ref, k16_ref, v16_ref, qb_ref, kb_ref, carry_ref) = refs
    else:
        (x_ref, g_ref, w_ref, wf_ref, bf_ref, q_ref, k_ref, v_ref, lf_ref) = refs
    d = d_model
    scale = (d // n_heads) ** -0.5
    hn = (_rms_scale(x_ref[0]) * g_ref[...]).astype(BF16)
    proj = _dot(hn, w_ref[...])
    k = proj[:, d:2 * d]
    v = proj[:, 2 * d:]
    q_ref[0] = (proj[:, :d] * scale).astype(BF16)
    k_ref[0] = k
    v_ref[0] = v
    f = _dot(hn, wf_ref[...]) + bf_ref[...]
    lane = lax.broadcasted_iota(jnp.int32, f.shape, 1)
    lf = jnp.where(lane < n_heads, _log_sigmoid(f), 0.0)
    lf_ref[0] = lf
    if with_bias:
        k16_ref[0] = k.astype(BF16)
        v16_ref[0] = v.astype(BF16)

        @pl.when(pl.program_id(1) == 0)
        def _():
            carry_ref[...] = jnp.zeros_like(carry_ref)

        cum, carry = _cumsum_rows(lf, carry_ref[...], _lower_tri(CUMSUM_BLOCK), CUMSUM_BLOCK)
        carry_ref[...] = carry
        packed = _pack3(cum)
        qb_ref[0] = _dot(packed, pq_ref[...]).astype(BF16)
        kb_ref[0] = _dot(packed, pk_ref[...]).astype(BF16)


def _proj_a(x, g, w_qkv, wf, bf, sel, *, n_heads, tm):
    b, s, d = x.shape
    with_bias = sel is not None
    row = lambda width: pl.BlockSpec((1, tm, width), lambda bi, i: (bi, i, 0))
    in_specs = [row(d), _const_spec((1, d)), _const_spec((d, 3 * d)),
                _const_spec((d, LANES)), _const_spec((1, LANES))]
    out_shape = [jax.ShapeDtypeStruct((b, s, d), BF16), jax.ShapeDtypeStruct((b, s, d), F32),
                 jax.ShapeDtypeStruct((b, s, d), F32), jax.ShapeDtypeStruct((b, s, LANES), F32)]
    out_specs = [row(d), row(d), row(d), row(LANES)]
    args = [x, g, w_qkv, wf, bf]
    scratch = []
    if with_bias:
        assert tm % CUMSUM_BLOCK == 0
        in_specs += [_const_spec((LANES, LANES))] * 2
        args += list(sel)
        out_shape += [jax.ShapeDtypeStruct((b, s, d), BF16), jax.ShapeDtypeStruct((b, s, d), BF16),
                      jax.ShapeDtypeStruct((b, s, LANES), BF16), jax.ShapeDtypeStruct((b, s, LANES), BF16)]
        out_specs += [row(d), row(d), row(LANES), row(LANES)]
        scratch = [pltpu.VMEM((1, LANES), F32)]
    return pl.pallas_call(
        functools.partial(_proj_a_kernel, d_model=d, n_heads=n_heads, with_bias=with_bias),
        grid=(b, s // tm), in_specs=in_specs, out_specs=out_specs, out_shape=out_shape,
        scratch_shapes=scratch, compiler_params=_params(2),
        name="fox_proj_bias" if with_bias else "fox_proj")(*args)


def _fox_attn_kernel(q_ref, qb_ref, k_ref, kb_ref, v_ref, o_ref, *, tq):
    pair = pl.program_id(1)
    qi = pl.program_id(2)
    lane_row = lax.broadcasted_iota(jnp.int32, (1, LANES), 1)
    row = lax.broadcasted_iota(jnp.int32, (tq, tq), 0)
    col = lax.broadcasted_iota(jnp.int32, (tq, tq), 1)
    causal = col <= row
    q = q_ref[0]
    qb = qb_ref[0]
    outs = []
    for hh in range(2):
        head = 2 * pair + hh
        q_mask = jnp.where(_div_pow2(lane_row, 64) == hh, 1.0, 0.0).astype(BF16)
        b_mask = jnp.where(_div_pow2(lane_row, BIAS_LANES_PER_HEAD) == head, 1.0, 0.0).astype(BF16)
        qa = jnp.concatenate([q * q_mask, qb * b_mask], axis=1)

        def step(kblk, carry, masked):
            m, l, acc = carry
            rows = pl.ds(pl.multiple_of(kblk * tq, tq), tq)
            ka = jnp.concatenate([k_ref[0, rows, :], kb_ref[0, rows, :]], axis=1)
            s = _dot_nt(qa, ka)
            if masked:
                s = jnp.where(causal, s, NEG_INF)
            m_new = jnp.maximum(m, jnp.max(s, axis=1, keepdims=True))
            alpha = jnp.exp(m - m_new)
            p = jnp.exp(s - m_new)
            l = alpha * l + jnp.sum(p, axis=1, keepdims=True)
            acc = alpha * acc + _dot(p.astype(BF16), v_ref[0, rows, :])
            return m_new, l, acc

        init = (jnp.full((tq, 1), NEG_INF, F32), jnp.zeros((tq, 1), F32),
                jnp.zeros((tq, LANES), F32))
        carry = lax.fori_loop(0, qi, lambda kblk, c: step(kblk, c, False), init)
        _, l, acc = step(qi, carry, True)
        outs.append(acc / l)
    lane = lax.broadcasted_iota(jnp.int32, (tq, LANES), 1)
    o_ref[0] = jnp.where(lane < 64, outs[0], outs[1]).astype(o_ref.dtype)


def _fox_attention(q, qb, k, kb, v, *, n_heads, tq):
    b, s, d = q.shape
    pairs = n_heads // 2
    blk = lambda rows, imap: pl.BlockSpec((1, rows, LANES), imap)
    return pl.pallas_call(
        functools.partial(_fox_attn_kernel, tq=tq),
        grid=(b, pairs, s // tq),
        in_specs=[blk(tq, lambda bi, p, i: (bi, i, p)), blk(tq, lambda bi, p, i: (bi, i, 0)),
                  blk(s, lambda bi, p, i: (bi, 0, p)), blk(s, lambda bi, p, i: (bi, 0, 0)),
                  blk(s, lambda bi, p, i: (bi, 0, p))],
        out_specs=blk(tq, lambda bi, p, i: (bi, i, p)),
        out_shape=jax.ShapeDtypeStruct((b, s, d), BF16),
        compiler_params=_params(3), name="fox_attention")(q, qb, k, kb, v)


FFN_CHUNK = 256


def _out_ffn_kernel(h_ref, a_ref, wo_ref, g_ref, wgu_ref, wd_ref, gf_ref, o_ref, *, d_ff, final_norm):
    h1 = h_ref[...] + _dot(a_ref[...], wo_ref[...])
    hn = (_rms_scale(h1) * g_ref[...]).astype(BF16)
    acc = h1
    for c in range(d_ff // FFN_CHUNK):
        gu = _dot(hn, wgu_ref[:, 2 * c * FFN_CHUNK:2 * (c + 1) * FFN_CHUNK])
        gate = gu[:, :FFN_CHUNK]
        act = (gate * jax.nn.sigmoid(gate) * gu[:, FFN_CHUNK:]).astype(BF16)
        acc = acc + _dot(act, wd_ref[c * FFN_CHUNK:(c + 1) * FFN_CHUNK, :])
    if final_norm:
        acc = _rms_scale(acc) * gf_ref[...]
    o_ref[...] = acc


def _out_ffn(h, attn, w_out, g_ffn, w_gu, w_down, g_final, *, final_norm, tm):
    m, d = h.shape
    d_ff = w_down.shape[0]
    row = lambda dt: pl.BlockSpec((tm, d), lambda i: (i, 0))
    return pl.pallas_call(
        functools.partial(_out_ffn_kernel, d_ff=d_ff, final_norm=final_norm),
        grid=(m // tm,),
        in_specs=[row(F32), row(BF16), _const_spec((d, d)), _const_spec((1, d)),
                  _const_spec((d, 2 * d_ff)), _const_spec((d_ff, d)), _const_spec((1, d))],
        out_specs=row(F32), out_shape=jax.ShapeDtypeStruct((m, d), F32),
        compiler_params=_params(1),
        name="out_ffn_final" if final_norm else "out_ffn")(h, attn, w_out, g_ffn, w_gu, w_down, g_final)


def _proj_b_kernel(*refs, d_model, n_heads, padded):
    if padded:
        h_ref, gq_ref, gkv_ref, wq_ref, wkv_ref, q_ref, k16_ref, v16_ref, kt_ref, vt_ref = refs
    else:
        h_ref, gq_ref, gkv_ref, wq_ref, wkv_ref, q_ref, k_ref, v_ref = refs
    d = d_model
    scale = (d // n_heads) ** -0.5
    r = _rms_scale(h_ref[0])
    q_ref[0] = (_dot((r * gq_ref[...]).astype(BF16), wq_ref[...]) * scale).astype(BF16)
    kv = _dot((r * gkv_ref[...]).astype(BF16), wkv_ref[...])
    k = kv[:, :d]
    v = kv[:, d:]
    if padded:
        j = pl.program_id(1)
        keep = j > 0
        k16_ref[0] = jnp.where(keep, k, 0.0).astype(BF16)
        v16_ref[0] = jnp.where(keep, v, 0.0).astype(BF16)

        @pl.when(j == pl.num_programs(1) - 1)
        def _():
            kt_ref[0] = k
            vt_ref[0] = v
    else:
        k_ref[0] = k
        v_ref[0] = v


def _proj_b(h, g_q, g_kv, w_q, w_kv, *, n_heads, tm, padded):
    b, s, d = h.shape
    consts = [_const_spec((1, d)), _const_spec((1, d)), _const_spec((d, d)), _const_spec((d, 2 * d))]
    if padded:
        assert tm == BAND_PAST and s % tm == 0
        src = lambda bi, j: (bi, jnp.maximum(j - 1, 0), 0)
        tail = pl.BlockSpec((1, tm, d), lambda bi, j: (bi, 0, 0))
        grid = (b, s // tm + 1)
        in_specs = [pl.BlockSpec((1, tm, d), src)] + consts
        out_specs = [pl.BlockSpec((1, tm, d), src), pl.BlockSpec((1, tm, d), lambda bi, j: (bi, j, 0)),
                     pl.BlockSpec((1, tm, d), lambda bi, j: (bi, j, 0)), tail, tail]
        out_shape = [jax.ShapeDtypeStruct((b, s, d), BF16),
                     jax.ShapeDtypeStruct((b, s + BAND_PAST, d), BF16),
                     jax.ShapeDtypeStruct((b, s + BAND_PAST, d), BF16),
                     jax.ShapeDtypeStruct((b, tm, d), F32), jax.ShapeDtypeStruct((b, tm, d), F32)]
    else:
        row = pl.BlockSpec((1, tm, d), lambda bi, j: (bi, j, 0))
        grid = (b, s // tm)
        in_specs = [row] + consts
        out_specs = [row, row, row]
        out_shape = [jax.ShapeDtypeStruct((b, s, d), BF16), jax.ShapeDtypeStruct((b, s, d), F32),
                     jax.ShapeDtypeStruct((b, s, d), F32)]
    return pl.pallas_call(
        functools.partial(_proj_b_kernel, d_model=d, n_heads=n_heads, padded=padded),
        grid=grid, in_specs=in_specs, out_specs=out_specs, out_shape=out_shape,
        compiler_params=_params(2),
        name="band_proj_padded" if padded else "band_proj")(h, g_q, g_kv, w_q, w_kv)


BAND_Q = 128
BAND_WIN = BAND_PAST + BAND_Q
TABLE_BASE = 768
REL_PAD = 384


def _bias_table_kernel(rb_ref, t_ref, *, n_heads):
    r = lax.broadcasted_iota(jnp.int32, (REL_PAD, TABLE_BASE), 0)
    u = lax.broadcasted_iota(jnp.int32, (REL_PAD, TABLE_BASE), 1)
    idx = jnp.clip(BAND_PAST + BAND_Q - 1 - u, -REL_CLIP, REL_CLIP) + REL_CLIP
    onehot = jnp.where(r == idx, 1.0, 0.0).astype(BF16)
    hi, mid, lo = _split3(rb_ref[...])
    base = _dot(hi, onehot) + _dot(mid, onehot) + _dot(lo, onehot)
    for h in range(n_heads):
        x = jnp.broadcast_to(base[h:h + 1, :], (BAND_Q, TABLE_BASE))
        y = pltpu.roll(x, TABLE_BASE - (BAND_Q - 1), 1, stride=1, stride_axis=0)
        t_ref[h] = y[:, :BAND_WIN]


def _bias_table(rel_bias):
    n_heads = rel_bias.shape[0]
    rb = jnp.pad(rel_bias, ((0, 0), (0, REL_PAD - rel_bias.shape[1])))
    return pl.pallas_call(
        functools.partial(_bias_table_kernel, n_heads=n_heads),
        out_shape=jax.ShapeDtypeStruct((n_heads, BAND_Q, BAND_WIN), F32),
        compiler_params=pltpu.CompilerParams(vmem_limit_bytes=VMEM_LIMIT_BYTES),
        name="rel_bias_table")(rb)


def _band_attn_kernel(q_ref, k_ref, v_ref, t_ref, o_ref):
    qi = pl.program_id(2)
    rows = pl.ds(pl.multiple_of(qi * BAND_Q, BAND_Q), BAND_WIN)
    kwin = k_ref[0, rows, :]
    vwin = v_ref[0, rows, :]
    q = q_ref[0]
    r = lax.broadcasted_iota(jnp.int32, (BAND_Q, BAND_WIN), 0)
    j = lax.broadcasted_iota(jnp.int32, (BAND_Q, BAND_WIN), 1)
    rel = j - _div_pow2(r, CHUNK) * CHUNK
    key_pos = qi * BAND_Q - BAND_PAST + j
    visible = (rel >= 0) & (rel < BAND_PAST + CHUNK) & (key_pos >= 0)
    lane_row = lax.broadcasted_iota(jnp.int32, (1, LANES), 1)
    outs = []
    for hh in range(2):
        q_mask = jnp.where(_div_pow2(lane_row, 64) == hh, 1.0, 0.0).astype(BF16)
        s = _dot_nt(q * q_mask, kwin) + t_ref[hh]
        s = jnp.where(visible, s, NEG_INF)
        m = jnp.max(s, axis=1, keepdims=True)
        p = jnp.exp(s - m)
        l = jnp.sum(p, axis=1, keepdims=True)
        outs.append(_dot(p.astype(BF16), vwin) / l)
    lane = lax.broadcasted_iota(jnp.int32, (BAND_Q, LANES), 1)
    o_ref[0] = jnp.where(lane < 64, outs[0], outs[1]).astype(o_ref.dtype)


def _band_attention(q, k_pad, v_pad, table, *, n_heads):
    b, s, d = q.shape
    sp = k_pad.shape[1]
    return pl.pallas_call(
        _band_attn_kernel,
        grid=(b, n_heads // 2, s // BAND_Q),
        in_specs=[pl.BlockSpec((1, BAND_Q, LANES), lambda bi, p, i: (bi, i, p)),
                  pl.BlockSpec((1, sp, LANES), lambda bi, p, i: (bi, 0, p)),
                  pl.BlockSpec((1, sp, LANES), lambda bi, p, i: (bi, 0, p)),
                  pl.BlockSpec((2, BAND_Q, BAND_WIN), lambda bi, p, i: (p, 0, 0))],
        out_specs=pl.BlockSpec((1, BAND_Q, LANES), lambda bi, p, i: (bi, i, p)),
        out_shape=jax.ShapeDtypeStruct((b, s, d), BF16),
        compiler_params=_params(3), name="band_attention")(q, k_pad, v_pad, table)


NEW_PAD = 128


def _stack_heads(x, n_heads, lanes_per_head):
    t, w = x.shape
    tiled = jnp.concatenate([x] * n_heads, axis=0)
    row_head = _div_pow2(lax.broadcasted_iota(jnp.int32, (n_heads * t, w), 0), t)
    lane_head = _div_pow2(lax.broadcasted_iota(jnp.int32, (n_heads * t, w), 1), lanes_per_head)
    return tiled * jnp.where(row_head == lane_head, 1.0, 0.0).astype(x.dtype)


def _unstack_heads(o, n_heads, t, head_dim):
    lane_head = _div_pow2(lax.broadcasted_iota(jnp.int32, (t, o.shape[1]), 1), head_dim)
    res = jnp.zeros((t, o.shape[1]), F32)
    for h in range(n_heads):
        res = jnp.where(lane_head == h, o[h * t:(h + 1) * t, :], res)
    return res


def _pad_rows(x, n):
    return jnp.concatenate([x, jnp.zeros((n - x.shape[0], x.shape[1]), x.dtype)], axis=0)


def _softmax_pv(s_c, s_n, v_c, v_n):
    m = jnp.maximum(jnp.max(s_c, axis=1, keepdims=True), jnp.max(s_n, axis=1, keepdims=True))
    p_c = jnp.exp(s_c - m)
    p_n = jnp.exp(s_n - m)
    l = jnp.sum(p_c, axis=1, keepdims=True) + jnp.sum(p_n, axis=1, keepdims=True)
    return (_dot(p_c.astype(BF16), v_c) + _dot(p_n.astype(BF16), v_n)) / l


def _fox_decode_kernel(q_ref, kn_ref, vn_ref, lfn_ref, ck_ref, cv_ref, clf_ref, pq_ref, pk_ref,
                       o_ref, *, n_heads, head_dim):
    t = q_ref.shape[1]
    past = ck_ref.shape[1]
    c_cache, total = _cumsum_rows(clf_ref[0], jnp.zeros((1, LANES), F32),
                                  _lower_tri(CUMSUM_BLOCK), CUMSUM_BLOCK)
    c_new, _ = _cumsum_rows(_pad_rows(lfn_ref[0], NEW_PAD), total, _lower_tri(NEW_PAD), NEW_PAD)
    packed_new = _pack3(c_new)
    kb_cache = _dot(_pack3(c_cache), pk_ref[...]).astype(BF16)
    kb_new = _dot(packed_new, pk_ref[...]).astype(BF16)
    qb_new = _dot(packed_new, pq_ref[...]).astype(BF16)[:t]

    qa = jnp.concatenate([_stack_heads(q_ref[0], n_heads, head_dim),
                          _stack_heads(qb_new, n_heads, BIAS_LANES_PER_HEAD)], axis=1)
    k_new = _pad_rows(kn_ref[0].astype(BF16), NEW_PAD)
    v_new = _pad_rows(vn_ref[0].astype(BF16), NEW_PAD)
    s_c = _dot_nt(qa, jnp.concatenate([ck_ref[0].astype(BF16), kb_cache], axis=1))
    s_n = _dot_nt(qa, jnp.concatenate([k_new, kb_new], axis=1))
    row_t = _mod_pow2(lax.broadcasted_iota(jnp.int32, s_n.shape, 0), t)
    col = lax.broadcasted_iota(jnp.int32, s_n.shape, 1)
    s_n = jnp.where(col <= row_t, s_n, NEG_INF)
    o = _softmax_pv(s_c, s_n, cv_ref[0].astype(BF16), v_new)
    o_ref[0] = _unstack_heads(o, n_heads, t, head_dim).astype(o_ref.dtype)


def _fox_decode_attention(q, k_new, v_new, lf_new, cache_k, cache_v, cache_lf, sel, *, n_heads):
    b, t, d = q.shape
    past = cache_k.shape[1]
    assert past % CUMSUM_BLOCK == 0 and t <= NEW_PAD and t % 16 == 0
    per_b = lambda rows, width: pl.BlockSpec((1, rows, width), lambda bi: (bi, 0, 0))
    return pl.pallas_call(
        functools.partial(_fox_decode_kernel, n_heads=n_heads, head_dim=d // n_heads),
        grid=(b,),
        in_specs=[per_b(t, d), per_b(t, d), per_b(t, d), per_b(t, LANES),
                  per_b(past, d), per_b(past, d), per_b(past, LANES),
                  _const_spec((LANES, LANES)), _const_spec((LANES, LANES))],
        out_specs=per_b(t, d), out_shape=jax.ShapeDtypeStruct((b, t, d), BF16),
        compiler_params=_params(1), name="fox_decode_attention")(
            q, k_new, v_new, lf_new, cache_k, cache_v, cache_lf, *sel)


def _band_decode_kernel(q_ref, kn_ref, vn_ref, ck_ref, cv_ref, t_ref, o_ref, nk_ref, nv_ref,
                        *, n_heads, head_dim, past):
    t = q_ref.shape[1]
    w = ck_ref.shape[1]
    qa = _stack_heads(q_ref[0], n_heads, head_dim)
    k_new = _pad_rows(kn_ref[0].astype(BF16), NEW_PAD)
    v_new = _pad_rows(vn_ref[0].astype(BF16), NEW_PAD)
    s_c = _dot_nt(qa, ck_ref[0].astype(BF16)) + t_ref[:, :w]
    s_n = _dot_nt(qa, k_new) + t_ref[:, w:w + NEW_PAD]

    def visible(shape, first_key):
        q_pos = past + _mod_pow2(lax.broadcasted_iota(jnp.int32, shape, 0), t)
        k_pos = first_key + lax.broadcasted_iota(jnp.int32, shape, 1)
        q_chunk = _div_pow2(q_pos, CHUNK)
        k_chunk = _div_pow2(k_pos, CHUNK)
        return (k_chunk <= q_chunk) & (k_chunk >= q_chunk - BAND_CHUNKS) & (k_pos < past + t)

    s_c = jnp.where(visible(s_c.shape, past - w), s_c, NEG_INF)
    s_n = jnp.where(visible(s_n.shape, past), s_n, NEG_INF)
    o = _softmax_pv(s_c, s_n, cv_ref[0].astype(BF16), v_new)
    o_ref[0] = _unstack_heads(o, n_heads, t, head_dim).astype(o_ref.dtype)
    nk_ref[0, :w - t, :] = ck_ref[0, t:, :]
    nk_ref[0, w - t:, :] = kn_ref[0]
    nv_ref[0, :w - t, :] = cv_ref[0, t:, :]
    nv_ref[0, w - t:, :] = vn_ref[0]


def _band_decode_attention(q, k_new, v_new, cache_k, cache_v, table2d, *, n_heads, past):
    b, t, d = q.shape
    w = cache_k.shape[1]
    assert t <= NEW_PAD and t % 16 == 0 and w % LANES == 0 and t <= w
    per_b = lambda rows: pl.BlockSpec((1, rows, d), lambda bi: (bi, 0, 0))
    return pl.pallas_call(
        functools.partial(_band_decode_kernel, n_heads=n_heads, head_dim=d // n_heads, past=past),
        grid=(b,),
        in_specs=[per_b(t), per_b(t), per_b(t), per_b(w), per_b(w),
                  _const_spec(table2d.shape)],
        out_specs=[per_b(t), per_b(w), per_b(w)],
        out_shape=[jax.ShapeDtypeStruct((b, t, d), BF16), jax.ShapeDtypeStruct((b, w, d), F32),
                   jax.ShapeDtypeStruct((b, w, d), F32)],
        compiler_params=_params(1), name="band_decode_attention")(
            q, k_new, v_new, cache_k, cache_v, table2d)


PROMPT_TILE = 512
DECODE_TILE = 128


def kernel(x_prompt, x_sample, cache_a_k, cache_a_v, cache_a_logf, cache_b_k, cache_b_v, g_attn, w_a_in, b_a_f, w_a_out, w_b_q, rel_bias, w_b_out, g_kv, w_kv, g_ffn, w_gate_up, w_down, g_final):
    b, s, d = x_prompt.shape
    db, dt, _ = x_sample.shape
    n_heads = b_a_f.shape[1]
    head_dim = d // n_heads
    d_ff = w_down.shape[1]
    past = cache_a_k.shape[2]
    band_w = cache_b_k.shape[1]
    depth = g_attn.shape[0]
    assert depth == 2 and cache_a_k.shape[0] == 1 and w_b_q.shape[0] == 1, "one FoX + one band layer"
    assert head_dim == 64 and n_heads == 16 and d == n_heads * head_dim
    assert s % PROMPT_TILE == 0 and s >= BAND_PAST and (db * dt) % DECODE_TILE == 0
    assert d_ff % FFN_CHUNK == 0 and band_w == BAND_PAST and band_w + NEW_PAD <= BAND_WIN

    w_qkv = w_a_in[0, :, :3 * d].astype(BF16)
    w_f = jnp.pad(w_a_in[0, :, 3 * d:], ((0, 0), (0, LANES - n_heads))).astype(BF16)
    b_f = jnp.pad(b_a_f[0], (0, LANES - n_heads)).reshape(1, LANES)
    n_chunks = d_ff // FFN_CHUNK
    w_gu = [jnp.stack([w_gate_up[l, :, :d_ff].reshape(d, n_chunks, FFN_CHUNK),
                       w_gate_up[l, :, d_ff:].reshape(d, n_chunks, FFN_CHUNK)], axis=2)
            .reshape(d, 2 * d_ff).astype(BF16) for l in range(depth)]
    w_dn = [w_down[l].astype(BF16) for l in range(depth)]
    w_ao = w_a_out[0].astype(BF16)
    w_bq = w_b_q[0].astype(BF16)
    w_bo = w_b_out[0].astype(BF16)
    w_kvb = w_kv.astype(BF16)
    row = lambda g: g.reshape(1, d)
    sel = _bias_select_matrices(n_heads)
    table = _bias_table(rel_bias[0])

    q, k, v, lf, k16, v16, qb, kb = _proj_a(x_prompt, row(g_attn[0]), w_qkv, w_f, b_f, sel,
                                            n_heads=n_heads, tm=PROMPT_TILE)
    attn = _fox_attention(q, qb, k16, kb, v16, n_heads=n_heads, tq=PROMPT_TILE)
    h = _out_ffn(x_prompt.reshape(b * s, d), attn.reshape(b * s, d), w_ao, row(g_ffn[0]),
                 w_gu[0], w_dn[0], row(g_final), final_norm=False, tm=PROMPT_TILE)
    qband, kpad, vpad, k_tail, v_tail = _proj_b(h.reshape(b, s, d), row(g_attn[1]), row(g_kv), w_bq, w_kvb,
                                                n_heads=n_heads, tm=PROMPT_TILE, padded=True)
    attn = _band_attention(qband, kpad, vpad, table, n_heads=n_heads)
    y_prompt = _out_ffn(h, attn.reshape(b * s, d), w_bo, row(g_ffn[1]), w_gu[1], w_dn[1],
                        row(g_final), final_norm=True, tm=PROMPT_TILE).reshape(b, s, d)
    heads5 = lambda a, n, t: a.reshape(1, n, t, n_heads, head_dim)
    new_a_k_p = heads5(k, b, s)
    new_a_v_p = heads5(v, b, s)
    new_a_logf_p = lf[:, :, :n_heads].reshape(1, b, s, n_heads)
    new_b_k_p = k_tail.reshape(b, BAND_PAST, n_heads, head_dim)
    new_b_v_p = v_tail.reshape(b, BAND_PAST, n_heads, head_dim)

    m = db * dt
    xs = x_sample.reshape(1, m, d)
    q, k, v, lf = _proj_a(xs, row(g_attn[0]), w_qkv, w_f, b_f, None, n_heads=n_heads, tm=DECODE_TILE)
    per_stream = lambda a: a.reshape(db, dt, a.shape[-1])
    cache_lf = jnp.pad(cache_a_logf[0], ((0, 0), (0, 0), (0, LANES - n_heads)))
    attn = _fox_decode_attention(per_stream(q), per_stream(k), per_stream(v), per_stream(lf),
                                 cache_a_k[0].reshape(db, past, d), cache_a_v[0].reshape(db, past, d),
                                 cache_lf, sel, n_heads=n_heads)
    hs = _out_ffn(x_sample.reshape(m, d), attn.reshape(m, d), w_ao, row(g_ffn[0]), w_gu[0], w_dn[0],
                  row(g_final), final_norm=False, tm=DECODE_TILE)
    qband, kb_new, vb_new = _proj_b(hs.reshape(1, m, d), row(g_attn[1]), row(g_kv), w_bq, w_kvb,
                                    n_heads=n_heads, tm=DECODE_TILE, padded=False)
    table2d = table[:, :dt, :].reshape(n_heads * dt, BAND_WIN)
    attn, nbk, nbv = _band_decode_attention(per_stream(qband), per_stream(kb_new), per_stream(vb_new),
                                            cache_b_k.reshape(db, band_w, d), cache_b_v.reshape(db, band_w, d),
                                            table2d, n_heads=n_heads, past=past)
    y_sample = _out_ffn(hs, attn.reshape(m, d), w_bo, row(g_ffn[1]), w_gu[1], w_dn[1],
                        row(g_final), final_norm=True, tm=DECODE_TILE).reshape(db, dt, d)
    new_a_k_s = heads5(k, db, dt)
    new_a_v_s = heads5(v, db, dt)
    new_a_logf_s = lf[0, :, :n_heads].reshape(1, db, dt, n_heads)
    new_b_k_s = nbk.reshape(db, band_w, n_heads, head_dim)
    new_b_v_s = nbv.reshape(db, band_w, n_heads, head_dim)

    return (y_prompt, y_sample, new_a_k_p, new_a_v_p, new_a_logf_p, new_b_k_p, new_b_v_p,
            new_a_k_s, new_a_v_s, new_a_logf_s, new_b_k_s, new_b_v_s)
```
